```python
import math, functools
import jax, jax.numpy as jnp
from jax import lax
import numpy as np

D_MODEL = 1024
BATCH = 4
SEQ = 8192
DEPTH = 2
DEC_BATCH = 32
DEC_SEQ = 1
PAST_LEN = 16384
PAGE_SIZE = 128

POOL_WINDOWS = (2, 4, 8, 16)
POOL_GROUPS = len(POOL_WINDOWS)
POOL_WIDTH = D_MODEL // 2
POOL_GDIM = POOL_WIDTH // POOL_GROUPS
POOL_STATE = max(POOL_WINDOWS) - 1
SGU_GROUPS = 4
SGU_WIDTH = D_MODEL // 2
SGU_GDIM = SGU_WIDTH // SGU_GROUPS
CHUNK = 128
ATT_HEADS = 4
ATT_WIDTH = D_MODEL // 2
ATT_V_DIM = ATT_WIDTH // ATT_HEADS
ATT_HALF_DIM = ATT_V_DIM // 2
ATT_QK_DIM = 2 * ATT_HALF_DIM
ROPE_DIM = ATT_HALF_DIM // 4
ROPE_THETA = 500000.0
Q_BLOCK = 128
N_BRANCH = 3
D_FF = 4 * D_MODEL
EPS = 1e-6
NEG = -1e30
OFF_POOL = 0
OFF_U = OFF_POOL + POOL_WIDTH
OFF_V = OFF_U + SGU_WIDTH
OFF_Q = OFF_V + SGU_WIDTH
OFF_K = OFF_Q + ATT_HEADS * ATT_QK_DIM
OFF_AV = OFF_K + ATT_HEADS * ATT_QK_DIM
OFF_GATE = OFF_AV + ATT_WIDTH
D_IN = OFF_GATE + N_BRANCH * D_MODEL

kernel_name = "hybrid_pool_sgu_diffattn_decode_step"

F32 = jnp.float32


def rmsnorm(x, g):
    xf = x.astype(F32)
    y = xf * lax.rsqrt(jnp.mean(xf * xf, axis=-1, keepdims=True) + EPS) * g.astype(F32)
    return y.astype(x.dtype)


def rope(x, pos):
    half = ROPE_DIM // 2
    inv = ROPE_THETA ** (-jnp.arange(half, dtype=F32) / half)
    ang = pos.astype(F32)[:, None] * inv[None, :]
    cos = jnp.cos(ang)[None, :, None, None, :]
    sin = jnp.sin(ang)[None, :, None, None, :]
    xr = x[..., :ROPE_DIM].astype(F32)
    x1, x2 = xr[..., :half], xr[..., half:]
    rot = jnp.concatenate([x1 * cos - x2 * sin, x2 * cos + x1 * sin], axis=-1).astype(x.dtype)
    return jnp.concatenate([rot, x[..., ROPE_DIM:]], axis=-1)


def pool_mix(a, buf, pos, w_pool, pool_scale):
    B, T, _ = a.shape
    P = buf.shape[1]
    full = jnp.concatenate([buf, a], axis=1)
    c = jnp.pad(jnp.cumsum(full.astype(F32), axis=1), ((0, 0), (1, 0), (0, 0)))
    idx_end = P + jnp.arange(T) + 1
    outs = []
    for g, w in enumerate(POOL_WINDOWS):
        sl = slice(g * POOL_GDIM, (g + 1) * POOL_GDIM)
        cnt = jnp.minimum(pos + 1, w)
        s = c[:, idx_end, sl] - c[:, idx_end - cnt, sl]
        outs.append(s / cnt.astype(F32)[None, :, None])
    pooled = jnp.stack(outs, axis=2) - a.astype(F32).reshape(B, T, POOL_GROUPS, POOL_GDIM)
    y = jnp.einsum('btgc,gcd->btgd', pooled, w_pool.astype(F32)).reshape(B, T, POOL_WIDTH)
    y = y * pool_scale.astype(F32)
    return y.astype(a.dtype), full[:, -POOL_STATE:]


def sgu_mix(u, v, ln_g, ln_b, w_s, b_s):
    B, T, _ = v.shape
    vf = v.astype(F32).reshape(B, T, SGU_GROUPS, SGU_GDIM)
    mu = jnp.mean(vf, axis=-1, keepdims=True)
    var = jnp.mean(jnp.square(vf - mu), axis=-1, keepdims=True)
    vn = (vf - mu) * lax.rsqrt(var + EPS) * ln_g.astype(F32).reshape(SGU_GROUPS, SGU_GDIM) \
        + ln_b.astype(F32).reshape(SGU_GROUPS, SGU_GDIM)
    L = min(T, CHUNK)
    nc = T // L
    vc = vn.reshape(B, nc, L, SGU_GROUPS, SGU_GDIM)
    ws = jnp.tril(w_s[:, :L, :L].astype(F32))
    s = jnp.einsum('gij,bnjgc->bnigc', ws, vc) + b_s[:, :L].astype(F32).T[None, None, :, :, None]
    out = u.astype(F32).reshape(B, nc, L, SGU_GROUPS, SGU_GDIM) * s
    return out.reshape(B, T, SGU_WIDTH).astype(u.dtype), vn.reshape(B, T, SGU_WIDTH).astype(v.dtype)


def diff_attn_prompt(q, k, v, lam):
    B, T = q.shape[:2]
    nb = T // Q_BLOCK
    scale = ATT_HALF_DIM ** -0.5
    qb = jnp.moveaxis(q.reshape(B, nb, Q_BLOCK, ATT_HEADS, 2, ATT_HALF_DIM), 1, 0)
    kpos = jnp.arange(T)
    vf = v.astype(F32)

    def block(args):
        qi, i = args
        s = jnp.einsum('bqhcd,bkhcd->bhcqk', qi, k, preferred_element_type=F32) * scale
        qpos = i * Q_BLOCK + jnp.arange(Q_BLOCK)
        s = jnp.where(kpos[None, :] <= qpos[:, None], s, NEG)
        p = jax.nn.softmax(s, axis=-1)
        a = p[:, :, 0] - lam * p[:, :, 1]
        return jnp.einsum('bhqk,bkhd->bqhd', a, vf)

    o = lax.map(block, (qb, jnp.arange(nb)))
    return jnp.moveaxis(o, 0, 1).reshape(B, T, ATT_HEADS, ATT_V_DIM)


def diff_attn_sample(q, k, v, lam, cache_k, cache_v, page_table, layer):
    Bd, T = q.shape[:2]
    past = page_table.shape[1] * PAGE_SIZE
    scale = ATT_HALF_DIM ** -0.5
    kp = cache_k[layer, page_table].reshape(Bd, past, ATT_HEADS, 2, ATT_HALF_DIM)
    vp = cache_v[layer, page_table].reshape(Bd, past, ATT_HEADS, ATT_V_DIM)
    s_past = jnp.einsum('bqhcd,bkhcd->bhcqk', q, kp, preferred_element_type=F32) * scale
    s_new = jnp.einsum('bqhcd,bkhcd->bhcqk', q, k, preferred_element_type=F32) * scale
    s_new = jnp.where(jnp.tril(jnp.ones((T, T), dtype=bool)), s_new, NEG)
    p = jax.nn.softmax(jnp.concatenate([s_past, s_new], axis=-1), axis=-1)
    a = p[:, :, 0] - lam * p[:, :, 1]
    o = jnp.einsum('bhqk,bkhd->bqhd', a[..., :past], vp.astype(F32)) \
        + jnp.einsum('bhqk,bkhd->bqhd', a[..., past:], v.astype(F32))
    return o


def layer_forward(x, pos, pool_buf, attend, p, lam_init):
    B, T, _ = x.shape
    h = rmsnorm(x, p['norm1_g'])
    z = jnp.einsum('btd,de->bte', h, p['w_in'])
    o_pool, pool_state = pool_mix(z[..., OFF_POOL:OFF_U], pool_buf, pos, p['pool_w'], p['pool_scale'])
    o_sgu, v_rows = sgu_mix(z[..., OFF_U:OFF_V], z[..., OFF_V:OFF_Q],
                            p['sgu_ln_g'], p['sgu_ln_b'], p['sgu_w'], p['sgu_b'])
    q = rope(rmsnorm(z[..., OFF_Q:OFF_K].reshape(B, T, ATT_HEADS, 2, ATT_HALF_DIM), p['q_norm_g']), pos)
    k = rope(rmsnorm(z[..., OFF_K:OFF_AV].reshape(B, T, ATT_HEADS, 2, ATT_HALF_DIM), p['k_norm_g']), pos)
    va = z[..., OFF_AV:OFF_GATE].reshape(B, T, ATT_HEADS, ATT_V_DIM)
    lam = (jnp.exp(jnp.sum(p['lam_q1'].astype(F32) * p['lam_k1'].astype(F32)))
           - jnp.exp(jnp.sum(p['lam_q2'].astype(F32) * p['lam_k2'].astype(F32))) + lam_init)
    o = attend(q, k, va, lam)
    o_att = (rmsnorm(o, p['attn_sub_g']) * (1.0 - lam_init)).reshape(B, T, ATT_WIDTH).astype(x.dtype)
    gates = jax.nn.sigmoid(z[..., OFF_GATE:].astype(F32)).reshape(B, T, N_BRANCH, D_MODEL)
    m = (gates[:, :, 0] * jnp.einsum('btc,cd->btd', o_pool, p['w_up_pool'], preferred_element_type=F32)
         + gates[:, :, 1] * jnp.einsum('btc,cd->btd', o_sgu, p['w_up_sgu'], preferred_element_type=F32)
         + gates[:, :, 2] * jnp.einsum('btc,cd->btd', o_att, p['w_up_att'], preferred_element_type=F32))
    x = x + jnp.einsum('btd,de->bte', m.astype(x.dtype), p['w_o'])
    h2 = rmsnorm(x, p['norm2_g'])
    f = jnp.square(jax.nn.relu(jnp.einsum('btd,df->btf', h2, p['w_ff1'])))
    x = x + jnp.einsum('btf,fd->btd', f, p['w_ff2'])
    k_rows = k.reshape(B, T, ATT_HEADS, ATT_QK_DIM)
    return x, pool_state, v_rows, k_rows, va


def setup_inputs(seed: int = 0) -> dict:
    key = jax.random.key(seed)
    ks = iter(jax.random.split(key, 40))
    nrm = lambda shape, s: jax.random.normal(next(ks), shape, F32) * s
    gain = lambda shape: 1.0 + 0.02 * jax.random.normal(next(ks), shape, F32)
    n_pages = PAST_LEN // PAGE_SIZE
    n_used = DEC_BATCH * n_pages
    n_pool = n_used + max(1, n_used // 4)
    inp = {}
    inp['x_prompt'] = nrm((BATCH, SEQ, D_MODEL), 1.0)
    inp['x_sample'] = nrm((DEC_BATCH, DEC_SEQ, D_MODEL), 1.0)
    inp['cache_k'] = nrm((DEPTH, n_pool, PAGE_SIZE, ATT_HEADS, ATT_QK_DIM), 1.0)
    inp['cache_v'] = nrm((DEPTH, n_pool, PAGE_SIZE, ATT_HEADS, ATT_V_DIM), 1.0)
    inp['state_pool'] = nrm((DEPTH, DEC_BATCH, POOL_STATE, POOL_WIDTH), 1.0)
    inp['page_table'] = jax.random.permutation(next(ks), n_pool)[:n_used].reshape(DEC_BATCH, n_pages).astype(jnp.int32)
    inp['norm1_g'] = gain((DEPTH, D_MODEL))
    inp['w_in'] = nrm((DEPTH, D_MODEL, D_IN), D_MODEL ** -0.5)
    inp['pool_w'] = nrm((DEPTH, POOL_GROUPS, POOL_GDIM, POOL_GDIM), POOL_GDIM ** -0.5)
    inp['pool_scale'] = gain((DEPTH, POOL_WIDTH))
    inp['sgu_ln_g'] = gain((DEPTH, SGU_WIDTH))
    inp['sgu_ln_b'] = nrm((DEPTH, SGU_WIDTH), 0.02)
    inp['sgu_w'] = nrm((DEPTH, SGU_GROUPS, CHUNK, CHUNK), CHUNK ** -0.5)
    inp['sgu_b'] = gain((DEPTH, SGU_GROUPS, CHUNK))
    inp['q_norm_g'] = gain((DEPTH, ATT_HALF_DIM))
    inp['k_norm_g'] = gain((DEPTH, ATT_HALF_DIM))
    inp['lam_q1'] = nrm((DEPTH, ATT_HALF_DIM), 0.1)
    inp['lam_k1'] = nrm((DEPTH, ATT_HALF_DIM), 0.1)
    inp['lam_q2'] = nrm((DEPTH, ATT_HALF_DIM), 0.1)
    inp['lam_k2'] = nrm((DEPTH, ATT_HALF_DIM), 0.1)
    inp['attn_sub_g'] = gain((DEPTH, ATT_V_DIM))
    inp['w_up_pool'] = nrm((DEPTH, POOL_WIDTH, D_MODEL), POOL_WIDTH ** -0.5)
    inp['w_up_sgu'] = nrm((DEPTH, SGU_WIDTH, D_MODEL), SGU_WIDTH ** -0.5)
    inp['w_up_att'] = nrm((DEPTH, ATT_WIDTH, D_MODEL), ATT_WIDTH ** -0.5)
    inp['w_o'] = nrm((DEPTH, D_MODEL, D_MODEL), D_MODEL ** -0.5)
    inp['norm2_g'] = gain((DEPTH, D_MODEL))
    inp['w_ff1'] = nrm((DEPTH, D_MODEL, D_FF), D_MODEL ** -0.5)
    inp['w_ff2'] = nrm((DEPTH, D_FF, D_MODEL), 0.5 * D_FF ** -0.5)
    return inp


def reference(x_prompt, x_sample, cache_k, cache_v, state_pool, page_table,
              norm1_g, w_in, pool_w, pool_scale, sgu_ln_g, sgu_ln_b, sgu_w, sgu_b,
              q_norm_g, k_norm_g, lam_q1, lam_k1, lam_q2, lam_k2, attn_sub_g,
              w_up_pool, w_up_sgu, w_up_att, w_o, norm2_g, w_ff1, w_ff2):
    past = page_table.shape[1] * PAGE_SIZE
    pos_p = jnp.arange(x_prompt.shape[1], dtype=jnp.int32)
    pos_s = past + jnp.arange(x_sample.shape[1], dtype=jnp.int32)
    xp, xs = x_prompt, x_sample
    kp_l, vp_l, pp_l, ks_l, vs_l, ps_l, us_l = [], [], [], [], [], [], []
    for l in range(DEPTH):
        lam_init = 0.8 - 0.6 * math.exp(-0.3 * l)
        p = dict(norm1_g=norm1_g[l], w_in=w_in[l], pool_w=pool_w[l], pool_scale=pool_scale[l],
                 sgu_ln_g=sgu_ln_g[l], sgu_ln_b=sgu_ln_b[l], sgu_w=sgu_w[l], sgu_b=sgu_b[l],
                 q_norm_g=q_norm_g[l], k_norm_g=k_norm_g[l], lam_q1=lam_q1[l], lam_k1=lam_k1[l],
                 lam_q2=lam_q2[l], lam_k2=lam_k2[l], attn_sub_g=attn_sub_g[l],
                 w_up_pool=w_up_pool[l], w_up_sgu=w_up_sgu[l], w_up_att=w_up_att[l], w_o=w_o[l],
                 norm2_g=norm2_g[l], w_ff1=w_ff1[l], w_ff2=w_ff2[l])
        empty_buf = jnp.zeros((xp.shape[0], 0, POOL_WIDTH), xp.dtype)
        xp, pool_p, _, k_p, v_p = layer_forward(xp, pos_p, empty_buf, diff_attn_prompt, p, lam_init)
        attend_s = functools.partial(diff_attn_sample, cache_k=cache_k, cache_v=cache_v,
                                     page_table=page_table, layer=l)
        xs, pool_s, u_s, k_s, v_s = layer_forward(xs, pos_s, state_pool[l], attend_s, p, lam_init)
        kp_l.append(k_p); vp_l.append(v_p); pp_l.append(pool_p)
        ks_l.append(k_s); vs_l.append(v_s); ps_l.append(pool_s); us_l.append(u_s)
    new_k_prompt = jnp.stack(kp_l)
    new_v_prompt = jnp.stack(vp_l)
    new_pool_prompt = jnp.stack(pp_l)
    new_k_sample = jnp.stack(ks_l)
    new_v_sample = jnp.stack(vs_l)
    new_pool_sample = jnp.stack(ps_l)
    new_sgu_v_sample = jnp.stack(us_l)
    return (xp, xs, new_k_prompt, new_v_prompt, new_pool_prompt,
            new_k_sample, new_v_sample, new_pool_sample, new_sgu_v_sample)
```

```python
import functools
import math

import numpy as np
import jax
import jax.numpy as jnp
from jax import lax
from jax.experimental import pallas as pl
from jax.experimental.pallas import tpu as pltpu

F32 = jnp.float32
BF16 = jnp.bfloat16

LANES = 128
POOL_WINDOWS = (2, 4, 8, 16)
N_GROUPS = 4
HALF = 64
ROPE_DIM = 16
ROPE_THETA = 500000.0
PAGE = 128
CHUNK = 128
EPS = 1e-6
NEG = -1e30
VMEM_LIMIT = 56 * 1024 * 1024


def _cparams(n_grid):
    return pltpu.CompilerParams(
        dimension_semantics=("arbitrary",) * n_grid,
        vmem_limit_bytes=VMEM_LIMIT)


def _const_spec(shape, n_grid):
    zeros = (0,) * len(shape)
    if n_grid == 1:
        imap = lambda i: zeros
    elif n_grid == 2:
        imap = lambda i, j: zeros
    else:
        imap = lambda i, j, k: zeros
    return pl.BlockSpec(shape, imap, pipeline_mode=pl.Buffered(1))


def _rms(x, g):
    ms = jnp.mean(x * x, axis=-1, keepdims=True)
    return x * lax.rsqrt(ms + EPS) * g


def _sigmoid(x):
    return 1.0 / (1.0 + jnp.exp(-x))


def _dot(a, b):
    return jnp.dot(a, b, preferred_element_type=F32)


def _qk_norm_rope(z, g_full, c, s1, s2):
    lane = lax.broadcasted_iota(jnp.int32, (1, LANES), 1)
    lo = lane < HALF
    outs = []
    for h in range(N_GROUPS):
        sl = slice(h * LANES, (h + 1) * LANES)
        x = z[:, sl]
        sq = x * x
        s_all = jnp.sum(sq, axis=-1, keepdims=True)
        s_lo = jnp.sum(jnp.where(lo, sq, 0.0), axis=-1, keepdims=True)
        ms = jnp.where(lo, s_lo, s_all - s_lo) * (1.0 / HALF)
        xn = x * lax.rsqrt(ms + EPS) * g_full[:, sl]
        xr = xn * c + pltpu.roll(xn, LANES - ROPE_DIM // 2, 1) * s1 \
            + pltpu.roll(xn, ROPE_DIM // 2, 1) * s2
        outs.append(xr)
    return jnp.concatenate(outs, axis=-1)


def _layer_norm_groups(v, ln_g, ln_b):
    outs = []
    for g in range(N_GROUPS):
        sl = slice(g * LANES, (g + 1) * LANES)
        vg = v[:, sl]
        mu = jnp.mean(vg, axis=-1, keepdims=True)
        d = vg - mu
        var = jnp.mean(d * d, axis=-1, keepdims=True)
        outs.append(d * lax.rsqrt(var + EPS) * ln_g[:, sl] + ln_b[:, sl])
    return outs


def _pool_project(pooled_groups, pool_w_ref, pool_scale):
    ys = [_dot(pooled_groups[g].astype(BF16), pool_w_ref[g]) for g in range(N_GROUPS)]
    return jnp.concatenate(ys, axis=-1) * pool_scale


def _prompt_proj_kernel(x_ref, n1_ref, win_ref, poolw_ref, pscale_ref, lng_ref, lnb_ref,
                        sguw_ref, sgubt_ref, qg_ref, kg_ref, c_ref, s1_ref, s2_ref,
                        wupp_ref, wups_ref,
                        k32_ref, v32_ref, qb_ref, kb_ref, vb_ref, mpart_ref, g2_ref, pstate_ref,
                        ext_ref, *, tm, n_tiles):
    j = pl.program_id(1)
    W = N_GROUPS * LANES
    D = x_ref.shape[-1]

    h = _rms(x_ref[0], n1_ref[...]).astype(BF16)

    def proj(off, width):
        return _dot(h, win_ref[:, off:off + width])

    @pl.when(j == 0)
    def _():
        ext_ref[0:16, :] = jnp.zeros((16, W), F32)

    a = proj(0, W)
    ext_ref[16:16 + tm, :] = a
    pos = j * tm + lax.broadcasted_iota(jnp.int32, (tm, 1), 0)
    pooled = []
    for g, w in enumerate(POOL_WINDOWS):
        sl = slice(g * LANES, (g + 1) * LANES)
        ag = a[:, sl]
        acc = ag
        for s in range(1, w):
            acc = acc + ext_ref[16 - s:16 - s + tm, sl]
        cnt = jnp.minimum(pos + 1, w).astype(F32)
        pooled.append(acc / cnt - ag)
    o_pool = _pool_project(pooled, poolw_ref, pscale_ref[...])

    @pl.when(j == n_tiles - 1)
    def _():
        pstate_ref[0] = ext_ref[tm + 1:tm + 16, :]

    ext_ref[0:16, :] = ext_ref[tm:tm + 16, :]

    u = proj(W, W)
    v = proj(2 * W, W)
    vn = _layer_norm_groups(v, lng_ref[...], lnb_ref[...])
    row = lax.broadcasted_iota(jnp.int32, (CHUNK, CHUNK), 0)
    col = lax.broadcasted_iota(jnp.int32, (CHUNK, CHUNK), 1)
    o_sgu_groups = []
    for g in range(N_GROUPS):
        sl = slice(g * LANES, (g + 1) * LANES)
        wsg = jnp.where(row >= col, sguw_ref[g], 0.0).astype(BF16)
        bias = sgubt_ref[:, g:g + 1]
        chunks = []
        for cidx in range(tm // CHUNK):
            rs = slice(cidx * CHUNK, (cidx + 1) * CHUNK)
            s = _dot(wsg, vn[g][rs].astype(BF16)) + bias
            chunks.append(u[rs, sl] * s)
        o_sgu_groups.append(jnp.concatenate(chunks, axis=0))
    o_sgu = jnp.concatenate(o_sgu_groups, axis=-1)

    c, s1, s2 = c_ref[...], s1_ref[...], s2_ref[...]
    q = _qk_norm_rope(proj(3 * W, W), qg_ref[...], c, s1, s2)
    k = _qk_norm_rope(proj(4 * W, W), kg_ref[...], c, s1, s2)
    va = proj(5 * W, W)
    k32_ref[0] = k
    v32_ref[0] = va
    qb_ref[0] = (q * (HALF ** -0.5)).astype(BF16)
    kb_ref[0] = k.astype(BF16)
    vb_ref[0] = va.astype(BF16)

    g0 = _sigmoid(proj(6 * W, D))
    g1 = _sigmoid(proj(6 * W + D, D))
    g2_ref[0] = _sigmoid(proj(6 * W + 2 * D, D))
    mpart_ref[0] = g0 * _dot(o_pool.astype(BF16), wupp_ref[...]) \
        + g1 * _dot(o_sgu.astype(BF16), wups_ref[...])


def _prompt_proj(x, lw, tables, tm):
    B, T, D = x.shape
    W = N_GROUPS * LANES
    DIN = lw['w_in'].shape[1]
    n_tiles = T // tm
    c_tab, s1_tab, s2_tab = tables
    row_spec = lambda width: pl.BlockSpec((1, tm, width), lambda b, j: (b, j, 0))
    tab_spec = pl.BlockSpec((tm, LANES), lambda b, j: (j, 0))
    cs = functools.partial(_const_spec, n_grid=2)
    in_specs = [
        row_spec(D), cs((1, D)), cs((D, DIN)), cs((N_GROUPS, LANES, LANES)), cs((1, W)),
        cs((1, W)), cs((1, W)), cs((N_GROUPS, CHUNK, CHUNK)), cs((CHUNK, N_GROUPS)),
        cs((1, W)), cs((1, W)), tab_spec, tab_spec, tab_spec,
        cs((W, D)), cs((W, D)),
    ]
    out_shape = (
        jax.ShapeDtypeStruct((B, T, W), F32), jax.ShapeDtypeStruct((B, T, W), F32),
        jax.ShapeDtypeStruct((B, T, W), BF16), jax.ShapeDtypeStruct((B, T, W), BF16),
        jax.ShapeDtypeStruct((B, T, W), BF16),
        jax.ShapeDtypeStruct((B, T, D), F32), jax.ShapeDtypeStruct((B, T, D), F32),
        jax.ShapeDtypeStruct((B, 15, W), F32),
    )
    out_specs = (
        row_spec(W), row_spec(W), row_spec(W), row_spec(W), row_spec(W),
        row_spec(D), row_spec(D),
        pl.BlockSpec((1, 15, W), lambda b, j: (b, 0, 0)),
    )
    return pl.pallas_call(
        functools.partial(_prompt_proj_kernel, tm=tm, n_tiles=n_tiles),
        grid=(B, n_tiles), in_specs=in_specs, out_specs=out_specs, out_shape=out_shape,
        scratch_shapes=[pltpu.VMEM((tm + 16, W), F32)],
        compiler_params=_cparams(2), name="prompt_proj",
    )(x, lw['norm1_g'], lw['w_in'], lw['pool_w'], lw['pool_scale'], lw['sgu_ln_g'],
      lw['sgu_ln_b'], lw['sgu_w'], lw['sgu_bt'], lw['q_g'], lw['k_g'],
      c_tab, s1_tab, s2_tab, lw['w_up_pool'], lw['w_up_sgu'])


def _lambda(lq1_ref, lk1_ref, lq2_ref, lk2_ref, lam_init):
    e1 = jnp.exp(jnp.sum(lq1_ref[...] * lk1_ref[...], axis=-1, keepdims=True))
    e2 = jnp.exp(jnp.sum(lq2_ref[...] * lk2_ref[...], axis=-1, keepdims=True))
    return e1 - e2 + lam_init


def _prompt_attn_kernel(qi_ref, ki_ref, q_ref, k_ref, v_ref, lq1_ref, lk1_ref, lq2_ref, lk2_ref,
                        sg_ref, o_ref, qs_ref, m_ref, l_ref, acc_ref, *, tq, lam_init):
    t = pl.program_id(2)
    i = qi_ref[t]
    kv = ki_ref[t]

    @pl.when(kv == 0)
    def _():
        q = q_ref[0]
        lane = lax.broadcasted_iota(jnp.int32, (1, LANES), 1)
        zero = jnp.zeros_like(q)
        qs_ref[0:tq, :] = jnp.where(lane < HALF, q, zero)
        qs_ref[tq:2 * tq, :] = jnp.where(lane >= HALF, q, zero)
        m_ref[...] = jnp.full(m_ref.shape, NEG, F32)
        l_ref[...] = jnp.zeros(l_ref.shape, F32)
        acc_ref[...] = jnp.zeros(acc_ref.shape, F32)

    def scores():
        return lax.dot_general(qs_ref[...], k_ref[0], (((1,), (1,)), ((), ())),
                               preferred_element_type=F32)

    def update(s):
        m_old = m_ref[...]
        m_new = jnp.maximum(m_old, jnp.max(s, axis=-1, keepdims=True))
        alpha = jnp.exp(m_old - m_new)
        p = jnp.exp(s - m_new[:, 0:1])
        l_ref[...] = alpha * l_ref[...] + jnp.sum(p, axis=-1, keepdims=True)
        acc_ref[...] = alpha * acc_ref[...] + _dot(p.astype(BF16), v_ref[0])
        m_ref[...] = m_new

    @pl.when(kv < i)
    def _():
        update(scores())

    @pl.when(kv == i)
    def _():
        s = scores()
        r = lax.broadcasted_iota(jnp.int32, (tq, tq), 0)
        cidx = lax.broadcasted_iota(jnp.int32, (tq, tq), 1)
        causal = jnp.concatenate([cidx <= r, cidx <= r], axis=0)
        update(jnp.where(causal, s, NEG))
        lam = _lambda(lq1_ref, lk1_ref, lq2_ref, lk2_ref, lam_init)
        o = acc_ref[0:tq, :] / l_ref[0:tq, :] - lam * (acc_ref[tq:2 * tq, :] / l_ref[tq:2 * tq, :])
        o_ref[0] = (_rms(o, sg_ref[...]) * (1.0 - lam_init)).astype(o_ref.dtype)


def _prompt_attn(qb, kb, vb, lw, lam_init, tq):
    B, T, W = qb.shape
    nq = T // tq
    qi = np.concatenate([np.full(i + 1, i, np.int32) for i in range(nq)])
    ki = np.concatenate([np.arange(i + 1, dtype=np.int32) for i in range(nq)])
    q_spec = pl.BlockSpec((1, tq, LANES), lambda b, h, t, qi, ki: (b, qi[t], h))
    kv_spec = pl.BlockSpec((1, tq, LANES), lambda b, h, t, qi, ki: (b, ki[t], h))
    small = lambda n: pl.BlockSpec((1, n), lambda b, h, t, qi, ki: (0, 0))
    grid_spec = pltpu.PrefetchScalarGridSpec(
        num_scalar_prefetch=2, grid=(B, N_GROUPS, len(qi)),
        in_specs=[q_spec, kv_spec, kv_spec, small(HALF), small(HALF), small(HALF), small(HALF),
                  small(LANES)],
        out_specs=q_spec,
        scratch_shapes=[pltpu.VMEM((2 * tq, LANES), BF16), pltpu.VMEM((2 * tq, LANES), F32),
                        pltpu.VMEM((2 * tq, LANES), F32), pltpu.VMEM((2 * tq, LANES), F32)])
    return pl.pallas_call(
        functools.partial(_prompt_attn_kernel, tq=tq, lam_init=lam_init),
        grid_spec=grid_spec, out_shape=jax.ShapeDtypeStruct((B, T, W), BF16),
        compiler_params=_cparams(3), name="prompt_attn",
    )(jnp.asarray(qi), jnp.asarray(ki), qb, kb, vb,
      lw['lam_q1'], lw['lam_k1'], lw['lam_q2'], lw['lam_k2'], lw['attn_sub_g'])


def _merge_ffn_kernel(x_ref, mpart_ref, g2_ref, oatt_ref, wupa_ref, wo_ref, n2_ref,
                      wff1_ref, wff2_ref, y_ref, *, ff_chunk):
    m = mpart_ref[...] + g2_ref[...] * _dot(oatt_ref[...].astype(BF16), wupa_ref[...])
    x1 = x_ref[...] + _dot(m.astype(BF16), wo_ref[...])
    h2 = _rms(x1, n2_ref[...]).astype(BF16)
    d_ff = wff1_ref.shape[1]
    y = x1
    for c in range(d_ff // ff_chunk):
        cs = slice(c * ff_chunk, (c + 1) * ff_chunk)
        f = jnp.maximum(_dot(h2, wff1_ref[:, cs]), 0.0)
        y = y + _dot((f * f).astype(BF16), wff2_ref[cs, :])
    y_ref[...] = y


def _merge_ffn(x, mpart, g2, oatt, lw, tm):
    N, D = x.shape
    W = oatt.shape[1]
    DFF = lw['w_ff1'].shape[1]
    row = lambda width: pl.BlockSpec((tm, width), lambda i: (i, 0))
    cs = functools.partial(_const_spec, n_grid=1)
    return pl.pallas_call(
        functools.partial(_merge_ffn_kernel, ff_chunk=min(DFF, 1024)),
        grid=(N // tm,),
        in_specs=[row(D), row(D), row(D), row(W), cs((W, D)), cs((D, D)), cs((1, D)),
                  cs((D, DFF)), cs((DFF, D))],
        out_specs=row(D), out_shape=jax.ShapeDtypeStruct((N, D), F32),
        compiler_params=_cparams(1), name="merge_ffn",
    )(x, mpart, g2, oatt, lw['w_up_att'], lw['w_o'], lw['norm2_g'], lw['w_ff1'], lw['w_ff2'])


def _sample_proj_kernel(x_ref, st_ref, n1_ref, win_ref, poolw_ref, pscale_ref, lng_ref, lnb_ref,
                        w00_ref, b0_ref, qg_ref, kg_ref, c_ref, s1_ref, s2_ref,
                        wupp_ref, wups_ref,
                        k32_ref, v32_ref, q32_ref, vn_ref, mpart_ref, g2_ref, nst_ref, *, past):
    W = N_GROUPS * LANES
    D = x_ref.shape[-1]
    n_state = st_ref.shape[0]
    h = _rms(x_ref[...], n1_ref[...]).astype(BF16)

    def proj(off, width):
        return _dot(h, win_ref[:, off:off + width])

    a = proj(0, W)
    pooled = []
    for g, w in enumerate(POOL_WINDOWS):
        sl = slice(g * LANES, (g + 1) * LANES)
        cnt = min(past + 1, w)
        acc = a[:, sl]
        for s in range(1, cnt):
            acc = acc + st_ref[n_state - s][:, sl]
        pooled.append(acc / float(cnt) - a[:, sl])
    o_pool = _pool_project(pooled, poolw_ref, pscale_ref[...])
    for r in range(n_state - 1):
        nst_ref[r] = st_ref[r + 1]
    nst_ref[n_state - 1] = a

    u = proj(W, W)
    vn = jnp.concatenate(_layer_norm_groups(proj(2 * W, W), lng_ref[...], lnb_ref[...]), axis=-1)
    vn_ref[...] = vn
    o_sgu = u * (w00_ref[...] * vn + b0_ref[...])

    c, s1, s2 = c_ref[...], s1_ref[...], s2_ref[...]
    q = _qk_norm_rope(proj(3 * W, W), qg_ref[...], c, s1, s2)
    k = _qk_norm_rope(proj(4 * W, W), kg_ref[...], c, s1, s2)
    k32_ref[...] = k
    v32_ref[...] = proj(5 * W, W)
    q32_ref[...] = q * (HALF ** -0.5)

    g0 = _sigmoid(proj(6 * W, D))
    g1 = _sigmoid(proj(6 * W + D, D))
    g2_ref[...] = _sigmoid(proj(6 * W + 2 * D, D))
    mpart_ref[...] = g0 * _dot(o_pool.astype(BF16), wupp_ref[...]) \
        + g1 * _dot(o_sgu.astype(BF16), wups_ref[...])


def _sample_proj(x, state_t, lw, tables, past):
    Bd, D = x.shape
    W = N_GROUPS * LANES
    sds = lambda *s: jax.ShapeDtypeStruct(s, F32)
    vm = pl.BlockSpec(memory_space=pltpu.VMEM)
    c_tab, s1_tab, s2_tab = tables
    args = (x, state_t, lw['norm1_g'], lw['w_in'], lw['pool_w'], lw['pool_scale'],
            lw['sgu_ln_g'], lw['sgu_ln_b'], lw['sgu_w00'], lw['sgu_b0'], lw['q_g'], lw['k_g'],
            c_tab, s1_tab, s2_tab, lw['w_up_pool'], lw['w_up_sgu'])
    return pl.pallas_call(
        functools.partial(_sample_proj_kernel, past=past),
        in_specs=[vm] * len(args), out_specs=(vm,) * 7,
        out_shape=(sds(Bd, W), sds(Bd, W), sds(Bd, W), sds(Bd, W), sds(Bd, D), sds(Bd, D),
                   sds(*state_t.shape)),
        compiler_params=pltpu.CompilerParams(vmem_limit_bytes=VMEM_LIMIT), name="sample_proj",
    )(*args)


def _segment_matrix():
    W = N_GROUPS * LANES
    r = np.arange(W)
    n = np.arange(2 * W)
    rh, rc = r // LANES, (r % LANES) // HALF
    nc, nh = n // W, (n % W) // LANES
    return ((rh[:, None] == nh[None, :]) & (rc[:, None] == nc[None, :])).astype(np.float32)


def _decode_attn_kernel(pt_ref, q_ref, kn_ref, vn_ref, seg_ref, lq1_ref, lk1_ref, lq2_ref, lk2_ref,
                        sg_ref, *rest, n_pages_step, lam_init):
    k_refs = rest[:n_pages_step]
    v_refs = rest[n_pages_step:2 * n_pages_step]
    o_ref, m_ref, l_ref, acc_ref = rest[2 * n_pages_step:]
    W = N_GROUPS * LANES
    j = pl.program_id(1)
    q = q_ref[0]

    @pl.when(j == 0)
    def _():
        prod = jnp.broadcast_to(kn_ref[0] * q, (8, W)).astype(BF16)
        s_new = _dot(prod, seg_ref[...])[0:1, :]
        first = lax.broadcasted_iota(jnp.int32, (8, 2 * W), 0) == 0
        vv = jnp.concatenate([vn_ref[0], vn_ref[0]], axis=-1)
        m_ref[...] = s_new
        l_ref[...] = jnp.where(first, 1.0, 0.0)
        acc_ref[...] = jnp.where(first, vv, 0.0)

    for p in range(n_pages_step):
        kp = k_refs[p][0]
        vp = v_refs[p][0]
        s = _dot((kp * q).astype(BF16), seg_ref[...])
        m_old = m_ref[...]
        m_blk = jnp.max(jnp.max(s.reshape(PAGE // 8, 8, 2 * W), axis=0), axis=0, keepdims=True)
        m_new = jnp.maximum(m_old, m_blk)
        alpha = jnp.exp(m_old - m_new)
        e = jnp.exp(s - m_new)
        ev = e * jnp.concatenate([vp, vp], axis=-1)
        l_ref[...] = alpha * l_ref[...] + jnp.sum(e.reshape(PAGE // 8, 8, 2 * W), axis=0)
        acc_ref[...] = alpha * acc_ref[...] + jnp.sum(ev.reshape(PAGE // 8, 8, 2 * W), axis=0)
        m_ref[...] = m_new

    @pl.when(j == pl.num_programs(1) - 1)
    def _():
        lsum = jnp.sum(l_ref[...], axis=0, keepdims=True)
        osum = jnp.sum(acc_ref[...], axis=0, keepdims=True)
        lam = _lambda(lq1_ref, lk1_ref, lq2_ref, lk2_ref, lam_init)
        o = osum[:, :W] / lsum[:, :W] - lam * (osum[:, W:] / lsum[:, W:])
        outs = []
        for h in range(N_GROUPS):
            sl = slice(h * LANES, (h + 1) * LANES)
            outs.append(_rms(o[:, sl], sg_ref[...]) * (1.0 - lam_init))
        o_ref[0] = jnp.concatenate(outs, axis=-1)


def _decode_attn(q, k_new, v_new, cache_k, cache_v, page_table, layer, n_pool, lw, lam_init):
    Bd, W = q.shape
    n_pages = page_table.shape[1]
    P = math.gcd(n_pages, 8)
    base = layer * n_pool
    row3 = pl.BlockSpec((1, 1, W), lambda b, j, pt: (b, 0, 0))
    small = lambda n: pl.BlockSpec((1, n), lambda b, j, pt: (0, 0))

    def page_spec(p):
        return pl.BlockSpec(
            (1, PAGE, W), lambda b, j, pt: (base + pt[b * n_pages + j * P + p], 0, 0))

    grid_spec = pltpu.PrefetchScalarGridSpec(
        num_scalar_prefetch=1, grid=(Bd, n_pages // P),
        in_specs=[row3, row3, row3, pl.BlockSpec((W, 2 * W), lambda b, j, pt: (0, 0)),
                  small(HALF), small(HALF), small(HALF), small(HALF), small(LANES)]
        + [page_spec(p) for p in range(P)] + [page_spec(p) for p in range(P)],
        out_specs=row3,
        scratch_shapes=[pltpu.VMEM((1, 2 * W), F32), pltpu.VMEM((8, 2 * W), F32),
                        pltpu.VMEM((8, 2 * W), F32)])
    seg = jnp.asarray(_segment_matrix(), BF16)
    out = pl.pallas_call(
        functools.partial(_decode_attn_kernel, n_pages_step=P, lam_init=lam_init),
        grid_spec=grid_spec, out_shape=jax.ShapeDtypeStruct((Bd, 1, W), F32),
        compiler_params=_cparams(2), name="decode_attn",
    )(page_table.reshape(-1), q.reshape(Bd, 1, W), k_new.reshape(Bd, 1, W),
      v_new.reshape(Bd, 1, W), seg, lw['lam_q1'], lw['lam_k1'], lw['lam_q2'], lw['lam_k2'],
      lw['attn_sub_g'], *([cache_k] * P), *([cache_v] * P))
    return out.reshape(Bd, W)


def _rope_tables(pos):
    half = ROPE_DIM // 2
    inv = ROPE_THETA ** (-jnp.arange(half, dtype=F32) / half)
    ang = pos.astype(F32)[:, None] * inv[None, :]
    cos, sin = jnp.cos(ang), jnp.sin(ang)
    n = pos.shape[0]
    z8 = jnp.zeros((n, half), F32)
    rest0 = jnp.zeros((n, HALF - ROPE_DIM), F32)
    rest1 = jnp.ones((n, HALF - ROPE_DIM), F32)
    c = jnp.concatenate([cos, cos, rest1], axis=-1)
    s1 = jnp.concatenate([-sin, z8, rest0], axis=-1)
    s2 = jnp.concatenate([z8, sin, rest0], axis=-1)
    rep = lambda t: jnp.concatenate([t, t], axis=-1)
    return rep(c), rep(s1), rep(s2)


def _pick_tile(n, pref):
    t = min(n, pref)
    while n % t:
        t //= 2
    return t


def kernel(x_prompt, x_sample, cache_k, cache_v, state_pool, page_table, norm1_g, w_in, pool_w, pool_scale, sgu_ln_g, sgu_ln_b, sgu_w, sgu_b, q_norm_g, k_norm_g, lam_q1, lam_k1, lam_q2, lam_k2, attn_sub_g, w_up_pool, w_up_sgu, w_up_att, w_o, norm2_g, w_ff1, w_ff2):
    B, T, D = x_prompt.shape
    Bd, Ts, _ = x_sample.shape
    depth, n_pool = cache_k.shape[0], cache_k.shape[1]
    W = N_GROUPS * LANES
    past = page_table.shape[1] * PAGE
    assert Ts == 1 and T % CHUNK == 0 and T >= 16

    tm_proj = _pick_tile(T, 256)
    tq = _pick_tile(T, 512)
    tm_ffn = _pick_tile(B * T, 512)

    tabs_p = _rope_tables(jnp.arange(T, dtype=jnp.int32))
    tabs_s = _rope_tables(past + jnp.arange(1, dtype=jnp.int32))
    ck = cache_k.reshape(depth * n_pool, PAGE, W)
    cv = cache_v.reshape(depth * n_pool, PAGE, W)

    xp = x_prompt
    xs = x_sample.reshape(Bd, D)
    kp_l, vp_l, pp_l, ks_l, vs_l, ps_l, us_l = [], [], [], [], [], [], []
    for l in range(depth):
        lam_init = 0.8 - 0.6 * math.exp(-0.3 * l)
        row = lambda a: a[l].reshape(1, -1)
        lw = dict(
            norm1_g=row(norm1_g), w_in=w_in[l].astype(BF16), pool_w=pool_w[l].astype(BF16),
            pool_scale=row(pool_scale), sgu_ln_g=row(sgu_ln_g), sgu_ln_b=row(sgu_ln_b),
            sgu_w=sgu_w[l], sgu_bt=sgu_b[l].T,
            sgu_w00=jnp.repeat(sgu_w[l][:, 0, 0], LANES).reshape(1, W),
            sgu_b0=jnp.repeat(sgu_b[l][:, 0], LANES).reshape(1, W),
            q_g=jnp.tile(q_norm_g[l], W // HALF).reshape(1, W),
            k_g=jnp.tile(k_norm_g[l], W // HALF).reshape(1, W),
            lam_q1=row(lam_q1), lam_k1=row(lam_k1), lam_q2=row(lam_q2), lam_k2=row(lam_k2),
            attn_sub_g=row(attn_sub_g),
            w_up_pool=w_up_pool[l].astype(BF16), w_up_sgu=w_up_sgu[l].astype(BF16),
            w_up_att=w_up_att[l].astype(BF16), w_o=w_o[l].astype(BF16), norm2_g=row(norm2_g),
            w_ff1=w_ff1[l].astype(BF16), w_ff2=w_ff2[l].astype(BF16))

        k32, v32, qb, kb, vb, mpart, g2, pstate = _prompt_proj(xp, lw, tabs_p, tm_proj)
        oatt = _prompt_attn(qb, kb, vb, lw, lam_init, tq)
        xp = _merge_ffn(xp.reshape(B * T, D), mpart.reshape(B * T, D), g2.reshape(B * T, D),
                        oatt.reshape(B * T, W), lw, tm_ffn).reshape(B, T, D)
        kp_l.append(k32.reshape(B, T, N_GROUPS, LANES))
        vp_l.append(v32.reshape(B, T, N_GROUPS, LANES))
        pp_l.append(pstate)

        state_t = jnp.transpose(state_pool[l], (1, 0, 2))
        ks, vs, qs, vn_s, mpart_s, g2_s, nstate_t = _sample_proj(xs, state_t, lw, tabs_s, past)
        oatt_s = _decode_attn(qs, ks, vs, ck, cv, page_table, l, n_pool, lw, lam_init)
        xs = _merge_ffn(xs, mpart_s, g2_s, oatt_s, lw, Bd)
        ks_l.append(ks.reshape(Bd, 1, N_GROUPS, LANES))
        vs_l.append(vs.reshape(Bd, 1, N_GROUPS, LANES))
        ps_l.append(jnp.transpose(nstate_t, (1, 0, 2)))
        us_l.append(vn_s.reshape(Bd, 1, W))

    return (xp, xs.reshape(Bd, 1, D), jnp.stack(kp_l), jnp.stack(vp_l), jnp.stack(pp_l),
            jnp.stack(ks_l), jnp.stack(vs_l), jnp.stack(ps_l), jnp.stack(us_l))
```

```python
import functools
import math

import numpy as np
import jax
import jax.numpy as jnp
from jax import lax
from jax.experimental import pallas as pl
from jax.experimental.pallas import tpu as pltpu

F32 = jnp.float32
BF16 = jnp.bfloat16

LANES = 128
POOL_WINDOWS = (2, 4, 8, 16)
N_GROUPS = 4
HALF = 64
ROPE_DIM = 16
ROPE_THETA = 500000.0
PAGE = 128
CHUNK = 128
EPS = 1e-6
NEG = -1e30
Q_SCALE = (HALF ** -0.5) * math.log2(math.e)
VMEM_LIMIT = 56 * 1024 * 1024


def _cparams(n_grid):
    return pltpu.CompilerParams(
        dimension_semantics=("arbitrary",) * n_grid,
        vmem_limit_bytes=VMEM_LIMIT)


def _const_spec(shape, n_grid):
    zeros = (0,) * len(shape)
    if n_grid == 1:
        imap = lambda i: zeros
    elif n_grid == 2:
        imap = lambda i, j: zeros
    else:
        imap = lambda i, j, k: zeros
    return pl.BlockSpec(shape, imap, pipeline_mode=pl.Buffered(1))


def _rms(x, g):
    ms = jnp.mean(x * x, axis=-1, keepdims=True)
    return x * lax.rsqrt(ms + EPS) * g


def _sigmoid(x):
    return 1.0 / (1.0 + jnp.exp(-x))


def _dot(a, b):
    return jnp.dot(a, b, preferred_element_type=F32)


def _qk_norm_rope(z, g_full, c, s1, s2):
    lane = lax.broadcasted_iota(jnp.int32, (1, LANES), 1)
    lo = lane < HALF
    outs = []
    for h in range(N_GROUPS):
        sl = slice(h * LANES, (h + 1) * LANES)
        x = z[:, sl]
        sq = x * x
        s_all = jnp.sum(sq, axis=-1, keepdims=True)
        s_lo = jnp.sum(jnp.where(lo, sq, 0.0), axis=-1, keepdims=True)
        ms = jnp.where(lo, s_lo, s_all - s_lo) * (1.0 / HALF)
        xn = x * lax.rsqrt(ms + EPS) * g_full[:, sl]
        xr = xn * c + pltpu.roll(xn, LANES - ROPE_DIM // 2, 1) * s1 \
            + pltpu.roll(xn, ROPE_DIM // 2, 1) * s2
        outs.append(xr)
    return jnp.concatenate(outs, axis=-1)


def _layer_norm_groups(v, ln_g, ln_b):
    outs = []
    for g in range(N_GROUPS):
        sl = slice(g * LANES, (g + 1) * LANES)
        vg = v[:, sl]
        mu = jnp.mean(vg, axis=-1, keepdims=True)
        d = vg - mu
        var = jnp.mean(d * d, axis=-1, keepdims=True)
        outs.append(d * lax.rsqrt(var + EPS) * ln_g[:, sl] + ln_b[:, sl])
    return outs


def _pool_project(pooled_groups, pool_w_ref, pool_scale):
    ys = [_dot(pooled_groups[g].astype(BF16), pool_w_ref[g]) for g in range(N_GROUPS)]
    return jnp.concatenate(ys, axis=-1) * pool_scale


def _prompt_proj_kernel(x_ref, n1_ref, win_ref, poolw_ref, pscale_ref, lng_ref, lnb_ref,
                        sguw_ref, sgubt_ref, qg_ref, kg_ref, c_ref, s1_ref, s2_ref,
                        wupp_ref, wups_ref,
                        k32_ref, v32_ref, qt_ref, kb_ref, vt_ref, mpart_ref, g2_ref, pstate_ref,
                        ext_ref, *, tm, n_tiles):
    j = pl.program_id(1)
    W = N_GROUPS * LANES
    D = x_ref.shape[-1]

    h = _rms(x_ref[0], n1_ref[...]).astype(BF16)

    def proj(off, width):
        return _dot(h, win_ref[:, off:off + width])

    @pl.when(j == 0)
    def _():
        ext_ref[0:16, :] = jnp.zeros((16, W), F32)

    a = proj(0, W)
    ext_ref[16:16 + tm, :] = a
    pos = j * tm + lax.broadcasted_iota(jnp.int32, (tm, 1), 0)
    pooled = []
    for g, w in enumerate(POOL_WINDOWS):
        sl = slice(g * LANES, (g + 1) * LANES)
        ag = a[:, sl]
        acc = ag
        for s in range(1, w):
            acc = acc + ext_ref[16 - s:16 - s + tm, sl]
        cnt = jnp.minimum(pos + 1, w).astype(F32)
        pooled.append(acc / cnt - ag)
    o_pool = _pool_project(pooled, poolw_ref, pscale_ref[...])

    @pl.when(j == n_tiles - 1)
    def _():
        pstate_ref[0] = ext_ref[tm + 1:tm + 16, :]

    ext_ref[0:16, :] = ext_ref[tm:tm + 16, :]

    u = proj(W, W)
    v = proj(2 * W, W)
    vn = _layer_norm_groups(v, lng_ref[...], lnb_ref[...])
    row = lax.broadcasted_iota(jnp.int32, (CHUNK, CHUNK), 0)
    col = lax.broadcasted_iota(jnp.int32, (CHUNK, CHUNK), 1)
    o_sgu_groups = []
    for g in range(N_GROUPS):
        sl = slice(g * LANES, (g + 1) * LANES)
        wsg = jnp.where(row >= col, sguw_ref[g], 0.0).astype(BF16)
        bias = sgubt_ref[:, g:g + 1]
        chunks = []
        for cidx in range(tm // CHUNK):
            rs = slice(cidx * CHUNK, (cidx + 1) * CHUNK)
            s = _dot(wsg, vn[g][rs].astype(BF16)) + bias
            chunks.append(u[rs, sl] * s)
        o_sgu_groups.append(jnp.concatenate(chunks, axis=0))
    o_sgu = jnp.concatenate(o_sgu_groups, axis=-1)

    c, s1, s2 = c_ref[...], s1_ref[...], s2_ref[...]
    q = _qk_norm_rope(proj(3 * W, W), qg_ref[...], c, s1, s2)
    k = _qk_norm_rope(proj(4 * W, W), kg_ref[...], c, s1, s2)
    va = proj(5 * W, W)
    k32_ref[0] = k
    v32_ref[0] = va
    kb_ref[0] = k.astype(BF16)
    qs = q * Q_SCALE
    for hd in range(N_GROUPS):
        sl = slice(hd * LANES, (hd + 1) * LANES)
        qt_ref[0, hd] = qs[:, sl].T.astype(BF16)
        vt_ref[0, hd] = va[:, sl].T.astype(BF16)

    g0 = _sigmoid(proj(6 * W, D))
    g1 = _sigmoid(proj(6 * W + D, D))
    g2_ref[0] = _sigmoid(proj(6 * W + 2 * D, D))
    mpart_ref[0] = g0 * _dot(o_pool.astype(BF16), wupp_ref[...]) \
        + g1 * _dot(o_sgu.astype(BF16), wups_ref[...])


def _prompt_proj(x, lw, tables, tm):
    B, T, D = x.shape
    W = N_GROUPS * LANES
    DIN = lw['w_in'].shape[1]
    n_tiles = T // tm
    c_tab, s1_tab, s2_tab = tables
    row_spec = lambda width: pl.BlockSpec((1, tm, width), lambda b, j: (b, j, 0))
    tab_spec = pl.BlockSpec((tm, LANES), lambda b, j: (j, 0))
    cs = functools.partial(_const_spec, n_grid=2)
    in_specs = [
        row_spec(D), cs((1, D)), cs((D, DIN)), cs((N_GROUPS, LANES, LANES)), cs((1, W)),
        cs((1, W)), cs((1, W)), cs((N_GROUPS, CHUNK, CHUNK)), cs((CHUNK, N_GROUPS)),
        cs((1, W)), cs((1, W)), tab_spec, tab_spec, tab_spec,
        cs((W, D)), cs((W, D)),
    ]
    t_shape = jax.ShapeDtypeStruct((B, N_GROUPS, LANES, T), BF16)
    t_spec = pl.BlockSpec((1, N_GROUPS, LANES, tm), lambda b, j: (b, 0, 0, j))
    out_shape = (
        jax.ShapeDtypeStruct((B, T, W), F32), jax.ShapeDtypeStruct((B, T, W), F32),
        t_shape, jax.ShapeDtypeStruct((B, T, W), BF16), t_shape,
        jax.ShapeDtypeStruct((B, T, D), F32), jax.ShapeDtypeStruct((B, T, D), F32),
        jax.ShapeDtypeStruct((B, 15, W), F32),
    )
    out_specs = (
        row_spec(W), row_spec(W), t_spec, row_spec(W), t_spec,
        row_spec(D), row_spec(D),
        pl.BlockSpec((1, 15, W), lambda b, j: (b, 0, 0)),
    )
    return pl.pallas_call(
        functools.partial(_prompt_proj_kernel, tm=tm, n_tiles=n_tiles),
        grid=(B, n_tiles), in_specs=in_specs, out_specs=out_specs, out_shape=out_shape,
        scratch_shapes=[pltpu.VMEM((tm + 16, W), F32)],
        compiler_params=_cparams(2), name="prompt_proj",
    )(x, lw['norm1_g'], lw['w_in'], lw['pool_w'], lw['pool_scale'], lw['sgu_ln_g'],
      lw['sgu_ln_b'], lw['sgu_w'], lw['sgu_bt'], lw['q_g'], lw['k_g'],
      c_tab, s1_tab, s2_tab, lw['w_up_pool'], lw['w_up_sgu'])


def _lambda(lq1_ref, lk1_ref, lq2_ref, lk2_ref, lam_init):
    e1 = jnp.exp(jnp.sum(lq1_ref[...] * lk1_ref[...], axis=-1, keepdims=True))
    e2 = jnp.exp(jnp.sum(lq2_ref[...] * lk2_ref[...], axis=-1, keepdims=True))
    return e1 - e2 + lam_init


ATTN_COL_GROUP = 2 * LANES


def _prompt_attn_kernel(qi_ref, ki_ref, qt_ref, k_ref, vt_ref, lq1_ref, lk1_ref, lq2_ref, lk2_ref,
                        sgc_ref, o_ref, qs_ref, m_ref, l_ref, acc_ref, *, tq, lam_init):
    t = pl.program_id(2)
    i = qi_ref[t]
    kv = ki_ref[t]
    gw = ATTN_COL_GROUP

    @pl.when(kv == 0)
    def _():
        qt = qt_ref[0, 0]
        feat = lax.broadcasted_iota(jnp.int32, qt.shape, 0)
        zero = jnp.zeros_like(qt)
        qs_ref[:, 0:tq] = jnp.where(feat < HALF, qt, zero)
        qs_ref[:, tq:2 * tq] = jnp.where(feat >= HALF, qt, zero)
        m_ref[...] = jnp.full(m_ref.shape, NEG, F32)
        l_ref[...] = jnp.zeros(l_ref.shape, F32)
        acc_ref[...] = jnp.zeros(acc_ref.shape, F32)

    def step(diagonal):
        kb = k_ref[0]
        vt = vt_ref[0, 0]
        n_groups = 2 * tq // gw

        def scores(g):
            return _dot(kb, qs_ref[:, g * gw:(g + 1) * gw])

        s_next = scores(0)
        for g in range(n_groups):
            cs = slice(g * gw, (g + 1) * gw)
            s = s_next
            if g + 1 < n_groups:
                s_next = scores(g + 1)
            if diagonal:
                key = lax.broadcasted_iota(jnp.int32, s.shape, 0)
                qry = (g * gw) % tq + lax.broadcasted_iota(jnp.int32, s.shape, 1)
                s = jnp.where(key <= qry, s, NEG)
            m_old = m_ref[:, cs]
            m_new = jnp.maximum(m_old, jnp.max(s, axis=0, keepdims=True))
            alpha = jnp.exp2(m_old - m_new)
            p = jnp.exp2(s - m_new)
            l_ref[:, cs] = alpha * l_ref[:, cs] + jnp.sum(p, axis=0, keepdims=True)
            acc_ref[:, cs] = alpha * acc_ref[:, cs] + _dot(vt, p.astype(BF16))
            m_ref[:, cs] = m_new

    @pl.when(kv < i)
    def _():
        step(False)

    @pl.when(kv == i)
    def _():
        step(True)
        lam = _lambda(lq1_ref, lk1_ref, lq2_ref, lk2_ref, lam_init)
        inv = 1.0 / l_ref[...]
        ot = acc_ref[:, 0:tq] * inv[:, 0:tq] - lam * (acc_ref[:, tq:2 * tq] * inv[:, tq:2 * tq])
        ms = jnp.mean(ot * ot, axis=0, keepdims=True)
        ot = ot * lax.rsqrt(ms + EPS) * sgc_ref[...] * (1.0 - lam_init)
        o_ref[0] = ot.T.astype(o_ref.dtype)


def _prompt_attn(qt, kb, vt, lw, lam_init, tq):
    B, T, W = kb.shape
    nq = T // tq
    qi = np.concatenate([np.full(i + 1, i, np.int32) for i in range(nq)])
    ki = np.concatenate([np.arange(i + 1, dtype=np.int32) for i in range(nq)])
    qt_spec = pl.BlockSpec((1, 1, LANES, tq), lambda b, h, t, qi, ki: (b, h, 0, qi[t]))
    vt_spec = pl.BlockSpec((1, 1, LANES, tq), lambda b, h, t, qi, ki: (b, h, 0, ki[t]))
    k_spec = pl.BlockSpec((1, tq, LANES), lambda b, h, t, qi, ki: (b, ki[t], h))
    o_spec = pl.BlockSpec((1, tq, LANES), lambda b, h, t, qi, ki: (b, qi[t], h))
    small = lambda r, n: pl.BlockSpec((r, n), lambda b, h, t, qi, ki: (0, 0))
    grid_spec = pltpu.PrefetchScalarGridSpec(
        num_scalar_prefetch=2, grid=(B, N_GROUPS, len(qi)),
        in_specs=[qt_spec, k_spec, vt_spec, small(1, HALF), small(1, HALF), small(1, HALF),
                  small(1, HALF), small(LANES, 1)],
        out_specs=o_spec,
        scratch_shapes=[pltpu.VMEM((LANES, 2 * tq), BF16), pltpu.VMEM((1, 2 * tq), F32),
                        pltpu.VMEM((1, 2 * tq), F32), pltpu.VMEM((LANES, 2 * tq), F32)])
    return pl.pallas_call(
        functools.partial(_prompt_attn_kernel, tq=tq, lam_init=lam_init),
        grid_spec=grid_spec, out_shape=jax.ShapeDtypeStruct((B, T, W), BF16),
        compiler_params=_cparams(3), name="prompt_attn",
    )(jnp.asarray(qi), jnp.asarray(ki), qt, kb, vt,
      lw['lam_q1'], lw['lam_k1'], lw['lam_q2'], lw['lam_k2'], lw['attn_sub_g'].reshape(LANES, 1))


def _merge_ffn_kernel(x_ref, mpart_ref, g2_ref, oatt_ref, wupa_ref, wo_ref, n2_ref,
                      wff1_ref, wff2_ref, y_ref, *, ff_chunk):
    m = mpart_ref[...] + g2_ref[...] * _dot(oatt_ref[...].astype(BF16), wupa_ref[...])
    x1 = x_ref[...] + _dot(m.astype(BF16), wo_ref[...])
    h2 = _rms(x1, n2_ref[...]).astype(BF16)
    d_ff = wff1_ref.shape[1]
    y = x1
    for c in range(d_ff // ff_chunk):
        cs = slice(c * ff_chunk, (c + 1) * ff_chunk)
        f = jnp.maximum(_dot(h2, wff1_ref[:, cs]), 0.0)
        y = y + _dot((f * f).astype(BF16), wff2_ref[cs, :])
    y_ref[...] = y


def _merge_ffn(x, mpart, g2, oatt, lw, tm):
    N, D = x.shape
    W = oatt.shape[1]
    DFF = lw['w_ff1'].shape[1]
    row = lambda width: pl.BlockSpec((tm, width), lambda i: (i, 0))
    cs = functools.partial(_const_spec, n_grid=1)
    return pl.pallas_call(
        functools.partial(_merge_ffn_kernel, ff_chunk=min(DFF, 1024)),
        grid=(N // tm,),
        in_specs=[row(D), row(D), row(D), row(W), cs((W, D)), cs((D, D)), cs((1, D)),
                  cs((D, DFF)), cs((DFF, D))],
        out_specs=row(D), out_shape=jax.ShapeDtypeStruct((N, D), F32),
        compiler_params=_cparams(1), name="merge_ffn",
    )(x, mpart, g2, oatt, lw['w_up_att'], lw['w_o'], lw['norm2_g'], lw['w_ff1'], lw['w_ff2'])


def _sample_proj_kernel(x_ref, st_ref, n1_ref, win_ref, poolw_ref, pscale_ref, lng_ref, lnb_ref,
                        w00_ref, b0_ref, qg_ref, kg_ref, c_ref, s1_ref, s2_ref,
                        wupp_ref, wups_ref,
                        k32_ref, v32_ref, q32_ref, vn_ref, mpart_ref, g2_ref, nst_ref, *, past):
    W = N_GROUPS * LANES
    D = x_ref.shape[-1]
    n_state = st_ref.shape[0]
    h = _rms(x_ref[...], n1_ref[...]).astype(BF16)

    def proj(off, width):
        return _dot(h, win_ref[:, off:off + width])

    a = proj(0, W)
    pooled = []
    for g, w in enumerate(POOL_WINDOWS):
        sl = slice(g * LANES, (g + 1) * LANES)
        cnt = min(past + 1, w)
        acc = a[:, sl]
        for s in range(1, cnt):
            acc = acc + st_ref[n_state - s][:, sl]
        pooled.append(acc / float(cnt) - a[:, sl])
    o_pool = _pool_project(pooled, poolw_ref, pscale_ref[...])
    for r in range(n_state - 1):
        nst_ref[r] = st_ref[r + 1]
    nst_ref[n_state - 1] = a

    u = proj(W, W)
    vn = jnp.concatenate(_layer_norm_groups(proj(2 * W, W), lng_ref[...], lnb_ref[...]), axis=-1)
    vn_ref[...] = vn
    o_sgu = u * (w00_ref[...] * vn + b0_ref[...])

    c, s1, s2 = c_ref[...], s1_ref[...], s2_ref[...]
    q = _qk_norm_rope(proj(3 * W, W), qg_ref[...], c, s1, s2)
    k = _qk_norm_rope(proj(4 * W, W), kg_ref[...], c, s1, s2)
    k32_ref[...] = k
    v32_ref[...] = proj(5 * W, W)
    q32_ref[...] = q * Q_SCALE

    g0 = _sigmoid(proj(6 * W, D))
    g1 = _sigmoid(proj(6 * W + D, D))
    g2_ref[...] = _sigmoid(proj(6 * W + 2 * D, D))
    mpart_ref[...] = g0 * _dot(o_pool.astype(BF16), wupp_ref[...]) \
        + g1 * _dot(o_sgu.astype(BF16), wups_ref[...])


def _sample_proj(x, state_t, lw, tables, past):
    Bd, D = x.shape
    W = N_GROUPS * LANES
    sds = lambda *s: jax.ShapeDtypeStruct(s, F32)
    vm = pl.BlockSpec(memory_space=pltpu.VMEM)
    c_tab, s1_tab, s2_tab = tables
    args = (x, state_t, lw['norm1_g'], lw['w_in'], lw['pool_w'], lw['pool_scale'],
            lw['sgu_ln_g'], lw['sgu_ln_b'], lw['sgu_w00'], lw['sgu_b0'], lw['q_g'], lw['k_g'],
            c_tab, s1_tab, s2_tab, lw['w_up_pool'], lw['w_up_sgu'])
    return pl.pallas_call(
        functools.partial(_sample_proj_kernel, past=past),
        in_specs=[vm] * len(args), out_specs=(vm,) * 7,
        out_shape=(sds(Bd, W), sds(Bd, W), sds(Bd, W), sds(Bd, W), sds(Bd, D), sds(Bd, D),
                   sds(*state_t.shape)),
        compiler_params=pltpu.CompilerParams(vmem_limit_bytes=VMEM_LIMIT), name="sample_proj",
    )(*args)


def _segment_matrix():
    r = np.arange(LANES)
    n = np.arange(2 * LANES)
    return (r[:, None] // HALF == n[None, :] // LANES).astype(np.float32)


PAGE_ROWS = PAGE * N_GROUPS


def _decode_attn_kernel(pt_ref, q_ref, kn_ref, vn_ref, seg_ref, lq1_ref, lk1_ref, lq2_ref, lk2_ref,
                        sg_ref, *rest, n_pages_step, lam_init):
    k_refs = rest[:n_pages_step]
    v_refs = rest[n_pages_step:2 * n_pages_step]
    o_ref, m_ref, l_ref, acc_ref = rest[2 * n_pages_step:]
    j = pl.program_id(1)
    n_sub = PAGE_ROWS // 8
    q4 = q_ref[0]
    qpat = jnp.concatenate([q4, q4], axis=0)

    @pl.when(j == 0)
    def _():
        kn = jnp.concatenate([kn_ref[0], kn_ref[0]], axis=0)
        vn = jnp.concatenate([vn_ref[0], vn_ref[0]], axis=0)
        top = lax.broadcasted_iota(jnp.int32, (8, 2 * LANES), 0) < N_GROUPS
        m_ref[...] = _dot((kn * qpat).astype(BF16), seg_ref[...])
        l_ref[...] = jnp.where(top, 1.0, 0.0)
        acc_ref[...] = jnp.where(top, jnp.concatenate([vn, vn], axis=-1), 0.0)

    def scores(p):
        kp = k_refs[p][...].reshape(n_sub, 8, LANES)
        prod = (kp * qpat[None]).reshape(PAGE_ROWS, LANES).astype(BF16)
        return _dot(prod, seg_ref[...]).reshape(n_sub, 8, 2 * LANES)

    m_run, l_run, acc_run = m_ref[...], l_ref[...], acc_ref[...]
    s_next = scores(0)
    for p in range(n_pages_step):
        s = s_next
        if p + 1 < n_pages_step:
            s_next = scores(p + 1)
        m_blk = jnp.max(s, axis=0)
        m_blk = jnp.maximum(m_blk, pltpu.roll(m_blk, N_GROUPS, 0))
        m_new = jnp.maximum(m_run, m_blk)
        alpha = jnp.exp2(m_run - m_new)
        e = jnp.exp2(s - m_new[None])
        vp = v_refs[p][...].reshape(n_sub, 8, LANES)
        pv = jnp.concatenate([jnp.sum(e[:, :, :LANES] * vp, axis=0),
                              jnp.sum(e[:, :, LANES:] * vp, axis=0)], axis=-1)
        l_run = alpha * l_run + jnp.sum(e, axis=0)
        acc_run = alpha * acc_run + pv
        m_run = m_new
    m_ref[...] = m_run
    l_ref[...] = l_run
    acc_ref[...] = acc_run

    @pl.when(j == pl.num_programs(1) - 1)
    def _():
        l4 = l_ref[0:N_GROUPS, :] + l_ref[N_GROUPS:8, :]
        a4 = acc_ref[0:N_GROUPS, :] + acc_ref[N_GROUPS:8, :]
        lam = _lambda(lq1_ref, lk1_ref, lq2_ref, lk2_ref, lam_init)
        o = a4[:, :LANES] / l4[:, :LANES] - lam * (a4[:, LANES:] / l4[:, LANES:])
        o_ref[0] = _rms(o, sg_ref[...]) * (1.0 - lam_init)


def _decode_attn(q, k_new, v_new, cache_k, cache_v, page_table, layer, n_pool, lw, lam_init):
    Bd = q.shape[0]
    n_pages = page_table.shape[1]
    P = math.gcd(n_pages, 8)
    base = layer * n_pool
    row3 = pl.BlockSpec((1, N_GROUPS, LANES), lambda b, j, pt: (b, 0, 0))
    small = lambda n: pl.BlockSpec((1, n), lambda b, j, pt: (0, 0))

    def page_spec(p):
        return pl.BlockSpec(
            (PAGE_ROWS, LANES), lambda b, j, pt: (base + pt[b * n_pages + j * P + p], 0))

    grid_spec = pltpu.PrefetchScalarGridSpec(
        num_scalar_prefetch=1, grid=(Bd, n_pages // P),
        in_specs=[row3, row3, row3, pl.BlockSpec((LANES, 2 * LANES), lambda b, j, pt: (0, 0)),
                  small(HALF), small(HALF), small(HALF), small(HALF), small(LANES)]
        + [page_spec(p) for p in range(P)] + [page_spec(p) for p in range(P)],
        out_specs=row3,
        scratch_shapes=[pltpu.VMEM((8, 2 * LANES), F32), pltpu.VMEM((8, 2 * LANES), F32),
                        pltpu.VMEM((8, 2 * LANES), F32)])
    seg = jnp.asarray(_segment_matrix(), BF16)
    return pl.pallas_call(
        functools.partial(_decode_attn_kernel, n_pages_step=P, lam_init=lam_init),
        grid_spec=grid_spec, out_shape=jax.ShapeDtypeStruct((Bd, N_GROUPS, LANES), F32),
        compiler_params=_cparams(2), name="decode_attn",
    )(page_table.reshape(-1), q, k_new, v_new, seg,
      lw['lam_q1'], lw['lam_k1'], lw['lam_q2'], lw['lam_k2'], lw['attn_sub_g'],
      *([cache_k] * P), *([cache_v] * P))


def _rope_tables(pos):
    half = ROPE_DIM // 2
    inv = ROPE_THETA ** (-jnp.arange(half, dtype=F32) / half)
    ang = pos.astype(F32)[:, None] * inv[None, :]
    cos, sin = jnp.cos(ang), jnp.sin(ang)
    n = pos.shape[0]
    z8 = jnp.zeros((n, half), F32)
    rest0 = jnp.zeros((n, HALF - ROPE_DIM), F32)
    rest1 = jnp.ones((n, HALF - ROPE_DIM), F32)
    c = jnp.concatenate([cos, cos, rest1], axis=-1)
    s1 = jnp.concatenate([-sin, z8, rest0], axis=-1)
    s2 = jnp.concatenate([z8, sin, rest0], axis=-1)
    rep = lambda t: jnp.concatenate([t, t], axis=-1)
    return rep(c), rep(s1), rep(s2)


def _pick_tile(n, pref):
    t = min(n, pref)
    while n % t:
        t //= 2
    return t


def kernel(x_prompt, x_sample, cache_k, cache_v, state_pool, page_table, norm1_g, w_in, pool_w, pool_scale, sgu_ln_g, sgu_ln_b, sgu_w, sgu_b, q_norm_g, k_norm_g, lam_q1, lam_k1, lam_q2, lam_k2, attn_sub_g, w_up_pool, w_up_sgu, w_up_att, w_o, norm2_g, w_ff1, w_ff2):
    B, T, D = x_prompt.shape
    Bd, Ts, _ = x_sample.shape
    depth, n_pool = cache_k.shape[0], cache_k.shape[1]
    W = N_GROUPS * LANES
    past = page_table.shape[1] * PAGE
    assert Ts == 1 and T % CHUNK == 0 and T >= 16

    tm_proj = _pick_tile(T, 256)
    tq = _pick_tile(T, 1024)
    tm_ffn = _pick_tile(B * T, 512)

    tabs_p = _rope_tables(jnp.arange(T, dtype=jnp.int32))
    tabs_s = _rope_tables(past + jnp.arange(1, dtype=jnp.int32))
    ck = cache_k.reshape(depth * n_pool * PAGE_ROWS, LANES)
    cv = cache_v.reshape(depth * n_pool * PAGE_ROWS, LANES)

    xp = x_prompt
    xs = x_sample.reshape(Bd, D)
    kp_l, vp_l, pp_l, ks_l, vs_l, ps_l, us_l = [], [], [], [], [], [], []
    for l in range(depth):
        lam_init = 0.8 - 0.6 * math.exp(-0.3 * l)
        row = lambda a: a[l].reshape(1, -1)
        lw = dict(
            norm1_g=row(norm1_g), w_in=w_in[l].astype(BF16), pool_w=pool_w[l].astype(BF16),
            pool_scale=row(pool_scale), sgu_ln_g=row(sgu_ln_g), sgu_ln_b=row(sgu_ln_b),
            sgu_w=sgu_w[l], sgu_bt=sgu_b[l].T,
            sgu_w00=jnp.repeat(sgu_w[l][:, 0, 0], LANES).reshape(1, W),
            sgu_b0=jnp.repeat(sgu_b[l][:, 0], LANES).reshape(1, W),
            q_g=jnp.tile(q_norm_g[l], W // HALF).reshape(1, W),
            k_g=jnp.tile(k_norm_g[l], W // HALF).reshape(1, W),
            lam_q1=row(lam_q1), lam_k1=row(lam_k1), lam_q2=row(lam_q2), lam_k2=row(lam_k2),
            attn_sub_g=row(attn_sub_g),
            w_up_pool=w_up_pool[l].astype(BF16), w_up_sgu=w_up_sgu[l].astype(BF16),
            w_up_att=w_up_att[l].astype(BF16), w_o=w_o[l].astype(BF16), norm2_g=row(norm2_g),
            w_ff1=w_ff1[l].astype(BF16), w_ff2=w_ff2[l].astype(BF16))

        k32, v32, qt, kb, vt, mpart, g2, pstate = _prompt_proj(xp, lw, tabs_p, tm_proj)
        oatt = _prompt_attn(qt, kb, vt, lw, lam_init, tq)
        xp = _merge_ffn(xp.reshape(B * T, D), mpart.reshape(B * T, D), g2.reshape(B * T, D),
                        oatt.reshape(B * T, W), lw, tm_ffn).reshape(B, T, D)
        kp_l.append(k32.reshape(B, T, N_GROUPS, LANES))
        vp_l.append(v32.reshape(B, T, N_GROUPS, LANES))
        pp_l.append(pstate)

        state_t = jnp.transpose(state_pool[l], (1, 0, 2))
        ks, vs, qs, vn_s, mpart_s, g2_s, nstate_t = _sample_proj(xs, state_t, lw, tabs_s, past)
        heads = lambda a: a.reshape(Bd, N_GROUPS, LANES)
        oatt_s = _decode_attn(heads(qs), heads(ks), heads(vs), ck, cv, page_table, l, n_pool,
                              lw, lam_init)
        xs = _merge_ffn(xs, mpart_s, g2_s, oatt_s.reshape(Bd, W), lw, Bd)
        ks_l.append(ks.reshape(Bd, 1, N_GROUPS, LANES))
        vs_l.append(vs.reshape(Bd, 1, N_GROUPS, LANES))
        ps_l.append(jnp.transpose(nstate_t, (1, 0, 2)))
        us_l.append(vn_s.reshape(Bd, 1, W))

    return (xp, xs.reshape(Bd, 1, D), jnp.stack(kp_l), jnp.stack(vp_l), jnp.stack(pp_l),
            jnp.stack(ks_l), jnp.stack(vs_l), jnp.stack(ps_l), jnp.stack(us_l))
```
